```python
import math, functools
import jax, jax.numpy as jnp
from jax import lax
import numpy as np

D_MODEL = 4096
BATCH = 4
SEQ = 2048
DEPTH = 1
DEC_BATCH = 32
DEC_SEQ = 1
PAST_LEN = 8192
PAGE_SIZE = 128

HA = D_MODEL // 512
DK = 128
DV = 2 * DK
WA = HA * DV
HG = D_MODEL // 256
DKH = 128
DVH = 128
WG = HG * DVH
D_MIX = WA + WG
Q_A = HA * 2 * DK
K_A = HA * 2 * DK
V_A = WA
Q_H = HG * DKH
F_H = HG * DKH
I_H = WG
G_H = WG
D_IN = Q_A + K_A + V_A + Q_H + F_H + I_H + G_H
IN_SPLITS = (Q_A, Q_A + K_A, Q_A + K_A + V_A, Q_A + K_A + V_A + Q_H,
             Q_A + K_A + V_A + Q_H + F_H, Q_A + K_A + V_A + Q_H + F_H + I_H)
N_EXPERTS = 32
TOP_K = 4
D_FF = D_MODEL
SWIGLU_LIMIT = 7.0
SWIGLU_ALPHA = 1.702
Q_BLOCK = 128
HG_CHUNK = 64
LN_EPS = 1e-5
RMS_EPS = 1e-5
DN_ALPHA = (2 * DEPTH) ** 0.25
DN_BETA = (8 * DEPTH) ** -0.25

kernel_name = 'hybrid_diffattn_hgrn2_moe_step'


def rms_norm(x, g):
    xf = x.astype(jnp.float32)
    return xf * lax.rsqrt(jnp.mean(xf * xf, axis=-1, keepdims=True) + RMS_EPS) * g.astype(jnp.float32)


def layer_norm(x, g, b):
    xf = x.astype(jnp.float32)
    mu = jnp.mean(xf, axis=-1, keepdims=True)
    xc = xf - mu
    var = jnp.mean(xc * xc, axis=-1, keepdims=True)
    return (xc * lax.rsqrt(var + LN_EPS) * g.astype(jnp.float32) + b.astype(jnp.float32)).astype(x.dtype)


def diff_attend(q, k, v, mask, lam):
    s = jnp.einsum('bqhcd,bkhcd->bhcqk', q.astype(jnp.float32), k.astype(jnp.float32)) * (DK ** -0.5)
    s = jnp.where(mask[:, None, None], s, -jnp.inf)
    p = jax.nn.softmax(s, axis=-1)
    a = p[:, :, 0] - lam * p[:, :, 1]
    return jnp.einsum('bhqk,bkhe->bqhe', a, v.astype(jnp.float32))


def prompt_attn(q, k, v, lam):
    B, S = q.shape[0], q.shape[1]
    nb = S // Q_BLOCK
    qb = q.reshape(B, nb, Q_BLOCK, HA, 2, DK).transpose(1, 0, 2, 3, 4, 5)
    kpos = jnp.arange(S)

    def block(args):
        qi, i = args
        qpos = i * Q_BLOCK + jnp.arange(Q_BLOCK)
        return diff_attend(qi, k, v, (kpos[None, :] <= qpos[:, None])[None], lam)

    o = lax.map(block, (qb, jnp.arange(nb)))
    return o.transpose(1, 0, 2, 3, 4).reshape(B, S, HA, DV)


def sample_attn(ck, cv, page_table, q, k, v, lam):
    T = q.shape[1]
    past = page_table.shape[1] * PAGE_SIZE
    kpos = jnp.arange(past + T)
    qpos = past + jnp.arange(T)
    mask = (kpos[None, :] <= qpos[:, None])[None]

    def one(args):
        qi, ki, vi, pt = args
        kk = jnp.concatenate([ck[pt].reshape(past, HA, 2, DK).astype(ki.dtype), ki], axis=0)
        vv = jnp.concatenate([cv[pt].reshape(past, HA, DV).astype(vi.dtype), vi], axis=0)
        return diff_attend(qi[None], kk[None], vv[None], mask, lam)[0]

    return lax.map(one, (q, k, v, page_table))


def hgrn2_chunked(q, logf, k, v, s0):
    B, T = q.shape[0], q.shape[1]
    C = min(HG_CHUNK, T)
    n = -(-T // C)
    pad = n * C - T

    def chunks(a):
        a = jnp.pad(a.astype(jnp.float32), ((0, 0), (0, pad), (0, 0), (0, 0)))
        return a.reshape(B, n, C, HG, a.shape[-1]).transpose(1, 0, 3, 2, 4)

    causal = jnp.tril(jnp.ones((C, C), dtype=bool))
    mid = C // 2

    def step(S, inp):
        qc, gc, kc, vc = inp
        G = jnp.cumsum(gc, axis=2)
        g_ref = G[:, :, mid:mid + 1]
        g_last = G[:, :, C - 1:]
        o_inter = jnp.einsum('bhck,bhkv->bhcv', qc * jnp.exp(G), S)
        A = jnp.einsum('bhtk,bhsk->bhts', qc * jnp.exp(G - g_ref), kc * jnp.exp(g_ref - G))
        A = jnp.where(causal, A, 0.0)
        o = o_inter + jnp.einsum('bhts,bhsv->bhtv', A, vc)
        S = jnp.exp(g_last)[:, :, 0, :, None] * S + jnp.einsum('bhsk,bhsv->bhkv', kc * jnp.exp(g_last - G), vc)
        return S, o

    S, o = lax.scan(step, s0.astype(jnp.float32), (chunks(q), chunks(logf), chunks(k), chunks(v)))
    o = o.transpose(1, 0, 3, 2, 4).reshape(B, n * C, HG, DVH)[:, :T]
    return o, S


def moe(x, w_router, b_router, w_gate_up, b_gate_up, w_down, b_down):
    B, T, D = x.shape
    xt = x.reshape(B * T, D)
    logits = (xt @ w_router + b_router).astype(jnp.float32)
    top_v, top_i = lax.top_k(logits, TOP_K)
    gates = jax.nn.softmax(top_v, axis=-1)
    dense = jnp.einsum('nk,nke->en', gates, jax.nn.one_hot(top_i, N_EXPERTS, dtype=jnp.float32))

    def expert(acc, inp):
        wgu, bgu, wd, bd, g = inp
        a, u = jnp.split(xt @ wgu + bgu, 2, axis=-1)
        a = jnp.minimum(a, SWIGLU_LIMIT)
        u = jnp.clip(u, -SWIGLU_LIMIT, SWIGLU_LIMIT)
        hdn = (u + 1.0) * a * jax.nn.sigmoid(SWIGLU_ALPHA * a)
        return acc + g[:, None] * (hdn @ wd + bd).astype(jnp.float32), None

    y, _ = lax.scan(expert, jnp.zeros((B * T, D), jnp.float32),
                    (w_gate_up, b_gate_up, w_down, b_down, dense))
    return y.astype(x.dtype).reshape(B, T, D)


def layer(l, x, attn_fn, s0, w_in, w_out, lam_q1, lam_k1, lam_q2, lam_k2, subln_g,
          lb_logits, hnorm_g, ln1_g, ln1_b, w_router, b_router, w_gate_up, b_gate_up,
          w_down, b_down, ln2_g, ln2_b):
    B, T, _ = x.shape
    h = jnp.einsum('btd,de->bte', x, w_in)
    qa, ka, va, qh, fh, ih, gh = jnp.split(h, IN_SPLITS, axis=-1)
    lam_init = 0.8 - 0.6 * math.exp(-0.3 * l)
    lam = (jnp.exp(jnp.sum(lam_q1.astype(jnp.float32) * lam_k1.astype(jnp.float32)))
           - jnp.exp(jnp.sum(lam_q2.astype(jnp.float32) * lam_k2.astype(jnp.float32))) + lam_init)
    qa = qa.reshape(B, T, HA, 2, DK)
    ka = ka.reshape(B, T, HA, 2, DK)
    va = va.reshape(B, T, HA, DV)
    oa = attn_fn(qa, ka, va, lam)
    oa = rms_norm(oa, subln_g) * (1.0 - lam_init)
    lb = jnp.cumsum(jax.nn.softmax(lb_logits.astype(jnp.float32), axis=0), axis=0)[l].reshape(HG, DKH)
    f = lb + (1.0 - lb) * jax.nn.sigmoid(fh.astype(jnp.float32).reshape(B, T, HG, DKH))
    oh, s_new = hgrn2_chunked(qh.reshape(B, T, HG, DKH), jnp.log(f), 1.0 - f,
                              ih.reshape(B, T, HG, DVH), s0)
    oh = rms_norm(oh * jax.nn.sigmoid(gh.astype(jnp.float32).reshape(B, T, HG, DVH)), hnorm_g)
    mix = jnp.concatenate([oa.reshape(B, T, WA), oh.reshape(B, T, WG)], axis=-1).astype(x.dtype)
    mix = jnp.einsum('bte,ed->btd', mix, w_out)
    x = layer_norm(DN_ALPHA * x + mix, ln1_g, ln1_b)
    x = layer_norm(DN_ALPHA * x + moe(x, w_router, b_router, w_gate_up, b_gate_up, w_down, b_down), ln2_g, ln2_b)
    return x, ka.reshape(B, T, HA, 2 * DK), va, s_new.astype(x.dtype)


def setup_inputs(seed: int = 0) -> dict:
    key = jax.random.key(seed)
    ks = jax.random.split(key, 32)
    f32 = jnp.float32

    def nrm(k, shape, s):
        return s * jax.random.normal(k, shape, f32)

    n_pages = PAST_LEN // PAGE_SIZE
    n_used = DEC_BATCH * n_pages
    n_pool = n_used + max(1, n_used // 4)
    col_scale = jnp.concatenate([jnp.ones((Q_A + K_A,), f32), jnp.full((V_A,), DN_BETA, f32),
                                 jnp.ones((Q_H + F_H,), f32), jnp.full((I_H,), DN_BETA, f32),
                                 jnp.ones((G_H,), f32)])
    return {
        'x_prompt': nrm(ks[0], (BATCH, SEQ, D_MODEL), 1.0),
        'x_sample': nrm(ks[1], (DEC_BATCH, DEC_SEQ, D_MODEL), 1.0),
        'cache_k': nrm(ks[2], (DEPTH, n_pool, PAGE_SIZE, HA, 2 * DK), 1.0),
        'cache_v': nrm(ks[3], (DEPTH, n_pool, PAGE_SIZE, HA, DV), 1.0),
        'state_hgrn': nrm(ks[4], (DEPTH, DEC_BATCH, HG, DKH, DVH), 0.5),
        'page_table': jax.random.permutation(ks[5], n_pool)[:n_used].reshape(DEC_BATCH, n_pages).astype(jnp.int32),
        'w_in': nrm(ks[6], (DEPTH, D_MODEL, D_IN), D_MODEL ** -0.5) * col_scale,
        'w_out': nrm(ks[7], (DEPTH, D_MIX, D_MODEL), DN_BETA * D_MIX ** -0.5),
        'lambda_q1': nrm(ks[8], (DEPTH, DK), 0.1),
        'lambda_k1': nrm(ks[9], (DEPTH, DK), 0.1),
        'lambda_q2': nrm(ks[10], (DEPTH, DK), 0.1),
        'lambda_k2': nrm(ks[11], (DEPTH, DK), 0.1),
        'subln_g': 1.0 + nrm(ks[12], (DEPTH, DV), 0.01),
        'hgrn_lb_logits': nrm(ks[13], (DEPTH + 1, HG * DKH), 0.5),
        'hgrn_norm_g': 1.0 + nrm(ks[14], (DEPTH, DVH), 0.01),
        'ln1_g': 1.0 + nrm(ks[15], (DEPTH, D_MODEL), 0.01),
        'ln1_b': nrm(ks[16], (DEPTH, D_MODEL), 0.01),
        'w_router': nrm(ks[17], (DEPTH, D_MODEL, N_EXPERTS), D_MODEL ** -0.5),
        'b_router': nrm(ks[18], (DEPTH, N_EXPERTS), 0.01),
        'w_gate_up': nrm(ks[19], (DEPTH, N_EXPERTS, D_MODEL, 2 * D_FF), D_MODEL ** -0.5),
        'b_gate_up': nrm(ks[20], (DEPTH, N_EXPERTS, 2 * D_FF), 0.01),
        'w_down': nrm(ks[21], (DEPTH, N_EXPERTS, D_FF, D_MODEL), DN_BETA * D_FF ** -0.5),
        'b_down': nrm(ks[22], (DEPTH, N_EXPERTS, D_MODEL), 0.01),
        'ln2_g': 1.0 + nrm(ks[23], (DEPTH, D_MODEL), 0.01),
        'ln2_b': nrm(ks[24], (DEPTH, D_MODEL), 0.01),
    }


def reference(x_prompt, x_sample, cache_k, cache_v, state_hgrn, page_table, w_in, w_out,
              lambda_q1, lambda_k1, lambda_q2, lambda_k2, subln_g, hgrn_lb_logits, hgrn_norm_g,
              ln1_g, ln1_b, w_router, b_router, w_gate_up, b_gate_up, w_down, b_down, ln2_g, ln2_b):
    yp, ys = x_prompt, x_sample
    kp, vp, sp, kd, vd, sd = [], [], [], [], [], []
    for l in range(DEPTH):
        wl = (w_in[l], w_out[l], lambda_q1[l], lambda_k1[l], lambda_q2[l], lambda_k2[l], subln_g[l],
              hgrn_lb_logits, hgrn_norm_g[l], ln1_g[l], ln1_b[l], w_router[l], b_router[l],
              w_gate_up[l], b_gate_up[l], w_down[l], b_down[l], ln2_g[l], ln2_b[l])
        s0 = jnp.zeros((yp.shape[0], HG, DKH, DVH), yp.dtype)
        yp, k_p, v_p, s_p = layer(l, yp, prompt_attn, s0, *wl)
        ys, k_d, v_d, s_d = layer(l, ys, functools.partial(sample_attn, cache_k[l], cache_v[l], page_table),
                                  state_hgrn[l], *wl)
        kp.append(k_p)
        vp.append(v_p)
        sp.append(s_p)
        kd.append(k_d)
        vd.append(v_d)
        sd.append(s_d)
    return (yp, ys, jnp.stack(kp), jnp.stack(vp), jnp.stack(sp), jnp.stack(kd), jnp.stack(vd), jnp.stack(sd))
```

```python
import functools
import math

import jax
import jax.numpy as jnp
from jax import lax
from jax.experimental import pallas as pl
from jax.experimental.pallas import tpu as pltpu

F32 = jnp.float32
BF16 = jnp.bfloat16

DEPTH = 1
HA = 8
DK = 128
DV = 256
HG = 16
DKH = 128
DVH = 128
N_EXPERTS = 32
TOP_K = 4
SWIGLU_LIMIT = 7.0
SWIGLU_ALPHA = 1.702
HG_CHUNK = 64
LN_EPS = 1e-5
RMS_EPS = 1e-5
DN_ALPHA = (2 * DEPTH) ** 0.25
PAGE_SIZE = 128

V7X_VMEM_LIMIT = 56 * 1024 * 1024
MOE_ROW_TILE = 256


def _cparams(sem):
    return pltpu.CompilerParams(dimension_semantics=sem, vmem_limit_bytes=V7X_VMEM_LIMIT)


def _mm_kernel(*refs, alpha, has_res):
    if has_res:
        x_ref, w_ref, res_ref, o_ref, wb_ref = refs
    else:
        x_ref, w_ref, o_ref, wb_ref = refs

    @pl.when(pl.program_id(1) == 0)
    def _():
        wb_ref[...] = w_ref[...].astype(BF16)

    acc = jnp.dot(x_ref[...], wb_ref[...], preferred_element_type=F32)
    if has_res:
        acc = alpha * res_ref[...] + acc
    o_ref[...] = acc.astype(o_ref.dtype)


def matmul(x, w, *, col_off=0, n_cols=None, tm, tn, res=None, alpha=1.0, out_dtype=F32):
    M, K = x.shape
    n_cols = w.shape[1] if n_cols is None else n_cols
    assert M % tm == 0 and n_cols % tn == 0 and col_off % tn == 0
    jo = col_off // tn
    in_specs = [pl.BlockSpec((tm, K), lambda j, i: (i, 0)),
                pl.BlockSpec((K, tn), lambda j, i: (0, jo + j))]
    args = [x, w]
    if res is not None:
        in_specs.append(pl.BlockSpec((tm, tn), lambda j, i: (i, j)))
        args.append(res)
    return pl.pallas_call(
        functools.partial(_mm_kernel, alpha=alpha, has_res=res is not None),
        grid=(n_cols // tn, M // tm),
        in_specs=in_specs,
        out_specs=pl.BlockSpec((tm, tn), lambda j, i: (i, j)),
        out_shape=jax.ShapeDtypeStruct((M, n_cols), out_dtype),
        scratch_shapes=[pltpu.VMEM((K, tn), BF16)],
        compiler_params=_cparams(("arbitrary", "arbitrary")),
        name="matmul",
    )(*args)


def _lambda_value(lam_ref, lam_init):
    a = jnp.sum(lam_ref[0:1, :] * lam_ref[1:2, :], axis=1, keepdims=True)
    b = jnp.sum(lam_ref[2:3, :] * lam_ref[3:4, :], axis=1, keepdims=True)
    return jnp.exp(a) - jnp.exp(b) + lam_init


def _sub_rmsnorm(o, g, post_scale):
    ms = jnp.mean(o * o, axis=-1, keepdims=True)
    return o * lax.rsqrt(ms + RMS_EPS) * g * post_scale


def _pattn_kernel(lam_ref, q_ref, k_ref, v_ref, g_ref, o_ref, *, tq, lam_init):
    qi = pl.program_id(2)
    lam = _lambda_value(lam_ref, lam_init)
    q = (q_ref[...] * (DK ** -0.5)).astype(BF16)
    q1, q2 = q[:, :DK], q[:, DK:]
    nt = (((1,), (1,)), ((), ()))

    def block(j, carry, masked):
        m1, l1, a1, m2, l2, a2 = carry
        rows = pl.ds(pl.multiple_of(j * tq, tq), tq)
        kb = k_ref[rows, :].astype(BF16)
        vb = v_ref[rows, :].astype(BF16)
        s1 = lax.dot_general(q1, kb[:, :DK], nt, preferred_element_type=F32)
        s2 = lax.dot_general(q2, kb[:, DK:], nt, preferred_element_type=F32)
        if masked:
            r = lax.broadcasted_iota(jnp.int32, (tq, tq), 0)
            c = lax.broadcasted_iota(jnp.int32, (tq, tq), 1)
            keep = c <= r
            s1 = jnp.where(keep, s1, -jnp.inf)
            s2 = jnp.where(keep, s2, -jnp.inf)

        def upd(s, m, l, a):
            mn = jnp.maximum(m, jnp.max(s, axis=1, keepdims=True))
            al = jnp.exp(m - mn)
            p = jnp.exp(s - mn)
            l = al * l + jnp.sum(p, axis=1, keepdims=True)
            a = al * a + jnp.dot(p.astype(BF16), vb, preferred_element_type=F32)
            return mn, l, a

        m1, l1, a1 = upd(s1, m1, l1, a1)
        m2, l2, a2 = upd(s2, m2, l2, a2)
        return m1, l1, a1, m2, l2, a2

    neg = jnp.full((tq, 1), -jnp.inf, F32)
    zl = jnp.zeros((tq, 1), F32)
    za = jnp.zeros((tq, DV), F32)
    carry = lax.fori_loop(0, qi, lambda j, c: block(j, c, False), (neg, zl, za, neg, zl, za))
    m1, l1, a1, m2, l2, a2 = block(qi, carry, True)
    o = a1 / l1 - lam * (a2 / l2)
    o_ref[...] = _sub_rmsnorm(o, g_ref[...], 1.0 - lam_init).astype(o_ref.dtype)


def prompt_attention(h3, lam4, subln_g, *, tq, lam_init):
    B, S, _ = h3.shape
    return pl.pallas_call(
        functools.partial(_pattn_kernel, tq=tq, lam_init=lam_init),
        grid=(B, HA, S // tq),
        in_specs=[pl.BlockSpec((4, DK), lambda b, h, i: (0, 0)),
                  pl.BlockSpec((None, tq, DV), lambda b, h, i: (b, i, h)),
                  pl.BlockSpec((None, S, DV), lambda b, h, i: (b, 0, HA + h)),
                  pl.BlockSpec((None, S, DV), lambda b, h, i: (b, 0, 2 * HA + h)),
                  pl.BlockSpec((1, DV), lambda b, h, i: (0, 0))],
        out_specs=pl.BlockSpec((None, tq, DV), lambda b, h, i: (b, i, h)),
        out_shape=jax.ShapeDtypeStruct((B, S, HA * DV), BF16),
        compiler_params=_cparams(("arbitrary", "arbitrary", "arbitrary")),
        name="prompt_attention",
    )(lam4, h3, h3, h3, subln_g)


def _sattn_kernel(pt_ref, lam_ref, q_ref, kn_ref, vn_ref, ones_ref, g_ref, ck_ref, cv_ref, o_ref,
                  m_ref, l_ref, a1_ref, a2_ref, *, lam_init):
    p = pl.program_id(1)
    qs = q_ref[...] * (DK ** -0.5)
    ones = ones_ref[...]

    def twice(x):
        return jnp.concatenate([x, x], axis=-1)

    @pl.when(p == 0)
    def _():
        s_self = jnp.dot((kn_ref[...] * qs).astype(BF16), ones, preferred_element_type=F32)
        m_ref[...] = s_self
        l_ref[...] = jnp.ones_like(s_self)
        a1_ref[...] = vn_ref[...]
        a2_ref[...] = vn_ref[...]

    k = ck_ref[...]
    prod = (k * qs[None]).astype(BF16).reshape(PAGE_SIZE * HA, 2 * DK)
    s = jnp.dot(prod, ones, preferred_element_type=F32).reshape(PAGE_SIZE, HA, 2 * DK)
    m_old = m_ref[...]
    m_new = jnp.maximum(m_old, jnp.max(s, axis=0))
    alpha = jnp.exp(m_old - m_new)
    pr = jnp.exp(s - m_new[None])
    l_ref[...] = alpha * l_ref[...] + jnp.sum(pr, axis=0)
    m_ref[...] = m_new
    v = cv_ref[...]
    a1_ref[...] = twice(alpha[:, :DK]) * a1_ref[...] + jnp.sum(twice(pr[:, :, :DK]) * v, axis=0)
    a2_ref[...] = twice(alpha[:, DK:]) * a2_ref[...] + jnp.sum(twice(pr[:, :, DK:]) * v, axis=0)

    @pl.when(p == pl.num_programs(1) - 1)
    def _():
        lam = _lambda_value(lam_ref, lam_init)
        l = l_ref[...]
        o = a1_ref[...] / twice(l[:, :DK]) - lam * (a2_ref[...] / twice(l[:, DK:]))
        o_ref[...] = _sub_rmsnorm(o, g_ref[...], 1.0 - lam_init).astype(o_ref.dtype)


def sample_attention(page_table, lam4, hs3, ones_blk, subln_g, cache_k, cache_v, *, lam_init):
    Bd, n_pages = page_table.shape
    grid_spec = pltpu.PrefetchScalarGridSpec(
        num_scalar_prefetch=1,
        grid=(Bd, n_pages),
        in_specs=[pl.BlockSpec((4, DK), lambda b, p, pt: (0, 0)),
                  pl.BlockSpec((None, HA, DV), lambda b, p, pt: (b, 0, 0)),
                  pl.BlockSpec((None, HA, DV), lambda b, p, pt: (b, 1, 0)),
                  pl.BlockSpec((None, HA, DV), lambda b, p, pt: (b, 2, 0)),
                  pl.BlockSpec((DV, DV), lambda b, p, pt: (0, 0)),
                  pl.BlockSpec((1, DV), lambda b, p, pt: (0, 0)),
                  pl.BlockSpec((None, PAGE_SIZE, HA, DV), lambda b, p, pt: (pt[b, p], 0, 0, 0)),
                  pl.BlockSpec((None, PAGE_SIZE, HA, DV), lambda b, p, pt: (pt[b, p], 0, 0, 0))],
        out_specs=pl.BlockSpec((None, HA, DV), lambda b, p, pt: (b, 0, 0)),
        scratch_shapes=[pltpu.VMEM((HA, DV), F32)] * 4,
    )
    return pl.pallas_call(
        functools.partial(_sattn_kernel, lam_init=lam_init),
        grid_spec=grid_spec,
        out_shape=jax.ShapeDtypeStruct((Bd, HA, DV), BF16),
        compiler_params=_cparams(("arbitrary", "arbitrary")),
        name="sample_attention",
    )(page_table, lam4, hs3, hs3, hs3, ones_blk, subln_g, cache_k, cache_v)


def _lower_bound(lb_ref, layer):
    z = lb_ref[...]
    e = jnp.exp(z - jnp.max(z, axis=0, keepdims=True))
    return jnp.sum(e[: layer + 1], axis=0, keepdims=True) / jnp.sum(e, axis=0, keepdims=True)


def _split3(x):
    hi = x.astype(BF16)
    r1 = x - hi.astype(F32)
    mid = r1.astype(BF16)
    lo = (r1 - mid.astype(F32)).astype(BF16)
    return hi, mid, lo


def _hgrn_prompt_kernel(q_ref, f_ref, i_ref, g_ref, lb_ref, gn_ref, o_ref, s_ref, st_ref,
                        *, hb, layer, seq):
    C = HG_CHUNK
    mid = C // 2
    lb = _lower_bound(lb_ref, layer)
    r = lax.broadcasted_iota(jnp.int32, (C, C), 0)
    c = lax.broadcasted_iota(jnp.int32, (C, C), 1)
    causal = c <= r
    tri = jnp.where(causal, 1.0, 0.0).astype(BF16)
    st_ref[...] = jnp.zeros_like(st_ref)
    nt = (((1,), (1,)), ((), ()))

    def chunk(ci, _):
        rows = pl.ds(pl.multiple_of(ci * C, C), C)
        for hh in range(hb):
            cols = slice(hh * DKH, (hh + 1) * DKH)
            qc = q_ref[rows, cols]
            f = lb[:, cols] + (1.0 - lb[:, cols]) * jax.nn.sigmoid(f_ref[rows, cols])
            gl = jnp.log(f)
            kc = 1.0 - f
            vc = i_ref[rows, cols]
            G = sum(jnp.dot(tri, t, preferred_element_type=F32) for t in _split3(gl))
            g_mid = G[mid:mid + 1, :]
            g_last = G[C - 1:C, :]
            st = st_ref[hh]
            o_inter = lax.dot_general((qc * jnp.exp(G)).astype(BF16), st.astype(BF16), nt,
                                      preferred_element_type=F32)
            A = lax.dot_general((qc * jnp.exp(G - g_mid)).astype(BF16),
                                (kc * jnp.exp(g_mid - G)).astype(BF16), nt,
                                preferred_element_type=F32)
            A = jnp.where(causal, A, 0.0)
            o = o_inter + jnp.dot(A.astype(BF16), vc.astype(BF16), preferred_element_type=F32)
            kd = (kc * jnp.exp(g_last - G)).astype(BF16)
            st_ref[hh] = jnp.exp(g_last) * st + jnp.dot(vc.T.astype(BF16), kd,
                                                        preferred_element_type=F32)
            og = o * jax.nn.sigmoid(g_ref[rows, cols])
            ms = jnp.mean(og * og, axis=-1, keepdims=True)
            o_ref[rows, cols] = (og * lax.rsqrt(ms + RMS_EPS) * gn_ref[...]).astype(o_ref.dtype)
        return 0

    lax.fori_loop(0, seq // C, chunk, 0)
    for hh in range(hb):
        s_ref[hh] = st_ref[hh].T


def hgrn_prompt(h3, lb_logits, hnorm_g, *, hb, layer):
    B, S, _ = h3.shape
    W = hb * DKH
    nb = HG // hb
    seg = lambda k: (lambda b, j: (b, 0, k * nb + j))
    return pl.pallas_call(
        functools.partial(_hgrn_prompt_kernel, hb=hb, layer=layer, seq=S),
        grid=(B, nb),
        in_specs=[pl.BlockSpec((None, S, W), seg(3)),
                  pl.BlockSpec((None, S, W), seg(4)),
                  pl.BlockSpec((None, S, W), seg(5)),
                  pl.BlockSpec((None, S, W), seg(6)),
                  pl.BlockSpec((lb_logits.shape[0], W), lambda b, j: (0, j)),
                  pl.BlockSpec((1, DVH), lambda b, j: (0, 0))],
        out_specs=[pl.BlockSpec((None, S, W), lambda b, j: (b, 0, j)),
                   pl.BlockSpec((None, hb, DKH, DVH), lambda b, j: (b, j, 0, 0))],
        out_shape=[jax.ShapeDtypeStruct((B, S, HG * DVH), BF16),
                   jax.ShapeDtypeStruct((B, HG, DKH, DVH), F32)],
        scratch_shapes=[pltpu.VMEM((hb, DVH, DKH), F32)],
        compiler_params=_cparams(("arbitrary", "arbitrary")),
        name="hgrn_prompt",
    )(h3, h3, h3, h3, lb_logits, hnorm_g)


def _hgrn_step_kernel(q_ref, f_ref, i_ref, g_ref, lb_ref, gn_ref, s0_ref, o_ref, s_ref, *, layer):
    z = lb_ref[...]
    e = jnp.exp(z - jnp.max(z, axis=0, keepdims=True))
    lb = jnp.sum(e[: layer + 1], axis=0) / jnp.sum(e, axis=0)
    q = q_ref[...]
    f = lb + (1.0 - lb) * jax.nn.sigmoid(f_ref[...])
    k = 1.0 - f
    v = i_ref[...]
    qk = jnp.sum(q * k, axis=1, keepdims=True)
    fT = f.T
    kT = k.T
    qfT = (q * f).T
    for h in range(HG):
        S = s0_ref[h]
        vh = v[h:h + 1, :]
        s_ref[h] = fT[:, h:h + 1] * S + kT[:, h:h + 1] * vh
        o = jnp.sum(qfT[:, h:h + 1] * S, axis=0, keepdims=True) + qk[h:h + 1, :] * vh
        og = o * jax.nn.sigmoid(g_ref[h:h + 1, :])
        ms = jnp.mean(og * og, axis=-1, keepdims=True)
        o_ref[h:h + 1, :] = (og * lax.rsqrt(ms + RMS_EPS) * gn_ref[...]).astype(o_ref.dtype)


def hgrn_step(hs4, lb3, hnorm_g, s0, *, layer):
    Bd = hs4.shape[0]
    seg = lambda k: (lambda b: (b, k, 0, 0))
    return pl.pallas_call(
        functools.partial(_hgrn_step_kernel, layer=layer),
        grid=(Bd,),
        in_specs=[pl.BlockSpec((None, None, HG, DKH), seg(3)),
                  pl.BlockSpec((None, None, HG, DKH), seg(4)),
                  pl.BlockSpec((None, None, HG, DVH), seg(5)),
                  pl.BlockSpec((None, None, HG, DVH), seg(6)),
                  pl.BlockSpec(lb3.shape, lambda b: (0, 0, 0)),
                  pl.BlockSpec((1, DVH), lambda b: (0, 0)),
                  pl.BlockSpec((None, HG, DKH, DVH), lambda b: (b, 0, 0, 0))],
        out_specs=[pl.BlockSpec((None, HG, DVH), lambda b: (b, 0, 0)),
                   pl.BlockSpec((None, HG, DKH, DVH), lambda b: (b, 0, 0, 0))],
        out_shape=[jax.ShapeDtypeStruct((Bd, HG, DVH), BF16),
                   jax.ShapeDtypeStruct((Bd, HG, DKH, DVH), F32)],
        compiler_params=_cparams(("arbitrary",)),
        name="hgrn_step",
    )(hs4, hs4, hs4, hs4, lb3, hnorm_g, s0)


def _layer_norm(z, g, b):
    mu = jnp.mean(z, axis=-1, keepdims=True)
    zc = z - mu
    var = jnp.mean(zc * zc, axis=-1, keepdims=True)
    return zc * lax.rsqrt(var + LN_EPS) * g + b


def _ln_router_kernel(z_ref, g_ref, b_ref, wr_ref, br_ref, x_ref, xb_ref, ti_ref, tg_ref):
    x = _layer_norm(z_ref[...], g_ref[...], b_ref[...])
    x_ref[...] = x
    xb_ref[...] = x.astype(BF16)
    w_hi, w_mid, w_lo = _split3(wr_ref[...])
    x_hi, x_mid, x_lo = _split3(x)
    d = lambda a, w: jnp.dot(a, w, preferred_element_type=F32)
    logits = (d(x_hi, w_hi) + (d(x_hi, w_mid) + d(x_mid, w_hi))
              + (d(x_hi, w_lo) + d(x_mid, w_mid) + d(x_lo, w_hi))) + br_ref[...]
    E = logits.shape[1]
    lane = lax.broadcasted_iota(jnp.int32, logits.shape, 1)
    col = lax.broadcasted_iota(jnp.int32, (logits.shape[0], TOP_K), 1)
    work = logits
    top_i = jnp.zeros((logits.shape[0], TOP_K), jnp.int32)
    top_v = jnp.zeros((logits.shape[0], TOP_K), F32)
    for r in range(TOP_K):
        mx = jnp.max(work, axis=1, keepdims=True)
        idx = jnp.min(jnp.where(work == mx, lane, E), axis=1, keepdims=True)
        top_i = jnp.where(col == r, idx, top_i)
        top_v = jnp.where(col == r, mx, top_v)
        work = jnp.where(lane == idx, -jnp.inf, work)
    ex = jnp.exp(top_v - top_v[:, 0:1])
    ti_ref[...] = top_i
    tg_ref[...] = ex / jnp.sum(ex, axis=1, keepdims=True)


def ln_router(z, g, b, w_router, b_router, *, tm):
    M, D = z.shape
    E = w_router.shape[1]
    row = lambda i: (i, 0)
    fix = lambda i: (0, 0)
    return pl.pallas_call(
        _ln_router_kernel,
        grid=(M // tm,),
        in_specs=[pl.BlockSpec((tm, D), row), pl.BlockSpec((1, D), fix), pl.BlockSpec((1, D), fix),
                  pl.BlockSpec((D, E), fix), pl.BlockSpec((1, E), fix)],
        out_specs=[pl.BlockSpec((tm, D), row), pl.BlockSpec((tm, D), row),
                   pl.BlockSpec((tm, TOP_K), row), pl.BlockSpec((tm, TOP_K), row)],
        out_shape=[jax.ShapeDtypeStruct((M, D), F32), jax.ShapeDtypeStruct((M, D), BF16),
                   jax.ShapeDtypeStruct((M, TOP_K), jnp.int32),
                   jax.ShapeDtypeStruct((M, TOP_K), F32)],
        compiler_params=_cparams(("arbitrary",)),
        name="ln_router",
    )(z, g, b, w_router, b_router)


def _ln_out_kernel(x_ref, y_ref, g_ref, b_ref, o_ref):
    o_ref[...] = _layer_norm(DN_ALPHA * x_ref[...] + y_ref[...], g_ref[...], b_ref[...])


def ln_out(x, y, g, b, *, tm, y_row_off=0):
    M, D = x.shape
    assert M % tm == 0 and y_row_off % tm == 0
    yo = y_row_off // tm
    row = lambda i: (i, 0)
    fix = lambda i: (0, 0)
    return pl.pallas_call(
        _ln_out_kernel,
        grid=(M // tm,),
        in_specs=[pl.BlockSpec((tm, D), row), pl.BlockSpec((tm, D), lambda i: (yo + i, 0)),
                  pl.BlockSpec((1, D), fix), pl.BlockSpec((1, D), fix)],
        out_specs=pl.BlockSpec((tm, D), row),
        out_shape=jax.ShapeDtypeStruct((M, D), F32),
        compiler_params=_cparams(("arbitrary",)),
        name="ln_out",
    )(x, y, g, b)


def _moe_up_kernel(te_ref, tf_ref, tv_ref, x_ref, wg_ref, wu_ref, bg_ref, bu_ref, o_ref,
                   wgb_ref, wub_ref):
    i = pl.program_id(1)

    @pl.when(tf_ref[i] == 1)
    def _():
        wgb_ref[...] = wg_ref[...].astype(BF16)
        wub_ref[...] = wu_ref[...].astype(BF16)

    @pl.when(tv_ref[i] == 1)
    def _():
        x = x_ref[...]
        a = jnp.dot(x, wgb_ref[...], preferred_element_type=F32) + bg_ref[...]
        u = jnp.dot(x, wub_ref[...], preferred_element_type=F32) + bu_ref[...]
        a = jnp.minimum(a, SWIGLU_LIMIT)
        u = jnp.clip(u, -SWIGLU_LIMIT, SWIGLU_LIMIT)
        o_ref[...] = ((u + 1.0) * a * jax.nn.sigmoid(SWIGLU_ALPHA * a)).astype(o_ref.dtype)

    @pl.when(tv_ref[i] == 0)
    def _():
        o_ref[...] = jnp.zeros_like(o_ref)


def moe_up(tile_e, tile_first, tile_valid, xs, w_gate_up, b_gate_up3, *, tm, tn):
    P, D = xs.shape
    E, _, F2 = w_gate_up.shape
    F = F2 // 2
    nj = F // tn
    grid_spec = pltpu.PrefetchScalarGridSpec(
        num_scalar_prefetch=3,
        grid=(nj, P // tm),
        in_specs=[pl.BlockSpec((tm, D), lambda j, i, te, tf, tv: (i, 0)),
                  pl.BlockSpec((None, D, tn), lambda j, i, te, tf, tv: (te[i], 0, j)),
                  pl.BlockSpec((None, D, tn), lambda j, i, te, tf, tv: (te[i], 0, nj + j)),
                  pl.BlockSpec((None, 1, tn), lambda j, i, te, tf, tv: (te[i], 0, j)),
                  pl.BlockSpec((None, 1, tn), lambda j, i, te, tf, tv: (te[i], 0, nj + j))],
        out_specs=pl.BlockSpec((tm, tn), lambda j, i, te, tf, tv: (i, j)),
        scratch_shapes=[pltpu.VMEM((D, tn), BF16), pltpu.VMEM((D, tn), BF16)],
    )
    return pl.pallas_call(
        _moe_up_kernel,
        grid_spec=grid_spec,
        out_shape=jax.ShapeDtypeStruct((P, F), BF16),
        compiler_params=_cparams(("arbitrary", "arbitrary")),
        name="moe_up",
    )(tile_e, tile_first, tile_valid, xs, w_gate_up, w_gate_up, b_gate_up3, b_gate_up3)


def _moe_down_kernel(te_ref, tf_ref, tv_ref, h_ref, wd_ref, bd_ref, gs_ref, o_ref, wdb_ref):
    i = pl.program_id(1)

    @pl.when(tf_ref[i] == 1)
    def _():
        wdb_ref[...] = wd_ref[...].astype(BF16)

    @pl.when(tv_ref[i] == 1)
    def _():
        y = jnp.dot(h_ref[...], wdb_ref[...], preferred_element_type=F32) + bd_ref[...]
        o_ref[...] = gs_ref[...] * y

    @pl.when(tv_ref[i] == 0)
    def _():
        o_ref[...] = jnp.zeros_like(o_ref)


def moe_down(tile_e, tile_first, tile_valid, hdn, w_down, b_down3, gate_slot, *, tm, tn):
    P, F = hdn.shape
    D = w_down.shape[2]
    grid_spec = pltpu.PrefetchScalarGridSpec(
        num_scalar_prefetch=3,
        grid=(D // tn, P // tm),
        in_specs=[pl.BlockSpec((tm, F), lambda j, i, te, tf, tv: (i, 0)),
                  pl.BlockSpec((None, F, tn), lambda j, i, te, tf, tv: (te[i], 0, j)),
                  pl.BlockSpec((None, 1, tn), lambda j, i, te, tf, tv: (te[i], 0, j)),
                  pl.BlockSpec((tm, 1), lambda j, i, te, tf, tv: (i, 0))],
        out_specs=pl.BlockSpec((tm, tn), lambda j, i, te, tf, tv: (i, j)),
        scratch_shapes=[pltpu.VMEM((F, tn), BF16)],
    )
    return pl.pallas_call(
        _moe_down_kernel,
        grid_spec=grid_spec,
        out_shape=jax.ShapeDtypeStruct((P, D), F32),
        compiler_params=_cparams(("arbitrary", "arbitrary")),
        name="moe_down",
    )(tile_e, tile_first, tile_valid, hdn, w_down, b_down3, gate_slot)


def _route(top_i, top_g, tm):
    N = top_i.shape[0]
    NK = N * TOP_K
    n_tiles = -(-NK // tm) + N_EXPERTS
    P = n_tiles * tm
    flat_e = top_i.reshape(NK)
    order = jnp.argsort(flat_e, stable=True).astype(jnp.int32)
    cnt = jnp.sum(flat_e[:, None] == jnp.arange(N_EXPERTS)[None, :], axis=0).astype(jnp.int32)
    start = jnp.cumsum(cnt) - cnt
    tiles_e = (cnt + tm - 1) // tm
    tend = jnp.cumsum(tiles_e)
    pstart = (tend - tiles_e) * tm
    total_tiles = tend[-1]
    t = jnp.arange(n_tiles, dtype=jnp.int32)
    te_raw = jnp.sum(t[:, None] >= tend[None, :], axis=1).astype(jnp.int32)
    tile_valid = (t < total_tiles).astype(jnp.int32)
    last_e = jnp.max(jnp.where(cnt > 0, jnp.arange(N_EXPERTS), 0)).astype(jnp.int32)
    tile_e = jnp.where(tile_valid == 1, jnp.minimum(te_raw, N_EXPERTS - 1), last_e)
    prev = jnp.concatenate([jnp.full((1,), -1, jnp.int32), tile_e[:-1]])
    tile_first = ((tile_e != prev) & (tile_valid == 1)).astype(jnp.int32)
    s = jnp.arange(P, dtype=jnp.int32)
    se = tile_e[s // tm]
    rnk = s - pstart[se]
    ok = (rnk < cnt[se]) & (tile_valid[s // tm] == 1)
    pair = order[jnp.clip(start[se] + rnk, 0, NK - 1)]
    tok_slot = jnp.where(ok, pair // TOP_K, 0)
    gate_slot = jnp.where(ok, top_g.reshape(NK)[pair], 0.0)
    pos = jnp.zeros((NK,), jnp.int32).at[order].set(jnp.arange(NK, dtype=jnp.int32))
    slot_pair = pstart[flat_e] + (pos - start[flat_e])
    return tile_e, tile_first, tile_valid, tok_slot, gate_slot, slot_pair.reshape(N, TOP_K)


def kernel(x_prompt, x_sample, cache_k, cache_v, state_hgrn, page_table, w_in, w_out, lambda_q1, lambda_k1, lambda_q2, lambda_k2, subln_g, hgrn_lb_logits, hgrn_norm_g, ln1_g, ln1_b, w_router, b_router, w_gate_up, b_gate_up, w_down, b_down, ln2_g, ln2_b):
    B, S, D = x_prompt.shape
    Bd = x_sample.shape[0]
    l = 0
    lam_init = 0.8 - 0.6 * math.exp(-0.3 * l)
    d_in = w_in.shape[2]
    wa = HA * DV

    xp = x_prompt.reshape(B * S, D)
    xd = x_sample.reshape(Bd, D)
    lam4 = jnp.stack([lambda_q1[l], lambda_k1[l], lambda_q2[l], lambda_k2[l]])
    g_sub = subln_g[l].reshape(1, DV)
    g_hn = hgrn_norm_g[l].reshape(1, DVH)

    hp = matmul(xp.astype(BF16), w_in[l], tm=1024, tn=512)
    hd = matmul(xd.astype(BF16), w_in[l], tm=Bd, tn=512)
    hp3 = hp.reshape(B, S, d_in)

    oa_p = prompt_attention(hp3, lam4, g_sub, tq=256, lam_init=lam_init)
    oh_p, s_p = hgrn_prompt(hp3, hgrn_lb_logits, g_hn, hb=2, layer=l)

    r = lax.broadcasted_iota(jnp.int32, (DV, DV), 0) // DK
    c = lax.broadcasted_iota(jnp.int32, (DV, DV), 1) // DK
    ones_blk = (r == c).astype(BF16)
    oa_d = sample_attention(page_table, lam4, hd.reshape(Bd, d_in // DV, DV), ones_blk, g_sub,
                            cache_k[l], cache_v[l], lam_init=lam_init)
    lb3 = hgrn_lb_logits.reshape(hgrn_lb_logits.shape[0], HG, DKH)
    oh_d, s_d = hgrn_step(hd.reshape(Bd, d_in // (HG * DKH), HG, DKH), lb3, g_hn,
                          state_hgrn[l], layer=l)

    N = B * S + Bd
    mix_p = jnp.concatenate([oa_p.reshape(B * S, wa), oh_p.reshape(B * S, HG * DVH)], axis=1)
    mix_d = jnp.concatenate([oa_d.reshape(Bd, wa), oh_d.reshape(Bd, HG * DVH)], axis=1)
    zp = matmul(mix_p, w_out[l], tm=1024, tn=512, res=xp, alpha=DN_ALPHA)
    zd = matmul(mix_d, w_out[l], tm=Bd, tn=512, res=xd, alpha=DN_ALPHA)
    ln1 = (ln1_g[l].reshape(1, D), ln1_b[l].reshape(1, D), w_router[l], b_router[l].reshape(1, -1))
    x1p, x1bp, ti_p, tg_p = ln_router(zp, *ln1, tm=256)
    x1d, x1bd, ti_d, tg_d = ln_router(zd, *ln1, tm=Bd)
    x1b = jnp.concatenate([x1bp, x1bd], axis=0)
    top_i = jnp.concatenate([ti_p, ti_d], axis=0)
    top_g = jnp.concatenate([tg_p, tg_d], axis=0)

    tm = MOE_ROW_TILE
    tile_e, tile_first, tile_valid, tok_slot, gate_slot, slot_pair = _route(top_i, top_g, tm)
    xs = jnp.take(x1b, tok_slot, axis=0)
    hdn = moe_up(tile_e, tile_first, tile_valid, xs, w_gate_up[l],
                 b_gate_up[l].reshape(N_EXPERTS, 1, -1), tm=tm, tn=512)
    ys = moe_down(tile_e, tile_first, tile_valid, hdn, w_down[l],
                  b_down[l].reshape(N_EXPERTS, 1, -1), gate_slot.reshape(-1, 1), tm=tm, tn=1024)
    y = jnp.sum(jnp.take(ys, slot_pair.reshape(-1), axis=0).reshape(N, TOP_K, D), axis=1)
    ln2 = (ln2_g[l].reshape(1, D), ln2_b[l].reshape(1, D))
    y_p = ln_out(x1p, y, *ln2, tm=256).reshape(B, S, D)
    y_d = ln_out(x1d, y, *ln2, tm=Bd, y_row_off=B * S).reshape(Bd, 1, D)

    k_p = hp[:, wa:2 * wa].reshape(1, B, S, HA, DV)
    v_p = hp[:, 2 * wa:3 * wa].reshape(1, B, S, HA, DV)
    k_d = hd[:, wa:2 * wa].reshape(1, Bd, 1, HA, DV)
    v_d = hd[:, 2 * wa:3 * wa].reshape(1, Bd, 1, HA, DV)
    return (y_p, y_d, k_p, v_p, s_p[None], k_d, v_d, s_d[None])
```

```python
import functools
import math

import jax
import jax.numpy as jnp
from jax import lax
from jax.experimental import pallas as pl
from jax.experimental.pallas import tpu as pltpu

F32 = jnp.float32
BF16 = jnp.bfloat16

DEPTH = 1
HA = 8
DK = 128
DV = 256
HG = 16
DKH = 128
DVH = 128
N_EXPERTS = 32
TOP_K = 4
SWIGLU_LIMIT = 7.0
SWIGLU_ALPHA = 1.702
HG_CHUNK = 64
LN_EPS = 1e-5
RMS_EPS = 1e-5
DN_ALPHA = (2 * DEPTH) ** 0.25
PAGE_SIZE = 128

V7X_VMEM_LIMIT = 56 * 1024 * 1024
MOE_GROUP_ROWS = 1536
MOE_CHUNK = 128


def _cparams(sem):
    return pltpu.CompilerParams(dimension_semantics=sem, vmem_limit_bytes=V7X_VMEM_LIMIT)


def _mm_kernel(*refs, alpha, has_res):
    if has_res:
        x_ref, w_ref, res_ref, o_ref, wb_ref = refs
    else:
        x_ref, w_ref, o_ref, wb_ref = refs

    @pl.when(pl.program_id(1) == 0)
    def _():
        wb_ref[...] = w_ref[...].astype(BF16)

    acc = jnp.dot(x_ref[...], wb_ref[...], preferred_element_type=F32)
    if has_res:
        acc = alpha * res_ref[...] + acc
    o_ref[...] = acc.astype(o_ref.dtype)


def matmul(x, w, *, col_off=0, n_cols=None, tm, tn, res=None, alpha=1.0, out_dtype=F32):
    M, K = x.shape
    n_cols = w.shape[1] if n_cols is None else n_cols
    assert M % tm == 0 and n_cols % tn == 0 and col_off % tn == 0
    jo = col_off // tn
    in_specs = [pl.BlockSpec((tm, K), lambda j, i: (i, 0)),
                pl.BlockSpec((K, tn), lambda j, i: (0, jo + j))]
    args = [x, w]
    if res is not None:
        in_specs.append(pl.BlockSpec((tm, tn), lambda j, i: (i, j)))
        args.append(res)
    return pl.pallas_call(
        functools.partial(_mm_kernel, alpha=alpha, has_res=res is not None),
        grid=(n_cols // tn, M // tm),
        in_specs=in_specs,
        out_specs=pl.BlockSpec((tm, tn), lambda j, i: (i, j)),
        out_shape=jax.ShapeDtypeStruct((M, n_cols), out_dtype),
        scratch_shapes=[pltpu.VMEM((K, tn), BF16)],
        compiler_params=_cparams(("arbitrary", "arbitrary")),
        name="matmul",
    )(*args)


def _lambda_value(lam_ref, lam_init):
    a = jnp.sum(lam_ref[0:1, :] * lam_ref[1:2, :], axis=1, keepdims=True)
    b = jnp.sum(lam_ref[2:3, :] * lam_ref[3:4, :], axis=1, keepdims=True)
    return jnp.exp(a) - jnp.exp(b) + lam_init


def _sub_rmsnorm(o, g, post_scale):
    ms = jnp.mean(o * o, axis=-1, keepdims=True)
    return o * lax.rsqrt(ms + RMS_EPS) * g * post_scale


def _pattn_kernel(lam_ref, q_ref, k_ref, v_ref, g_ref, o_ref, *, tq, lam_init):
    qi = pl.program_id(2)
    lam = _lambda_value(lam_ref, lam_init)
    q = (q_ref[...] * (DK ** -0.5)).astype(BF16)
    q1, q2 = q[:, :DK], q[:, DK:]
    nt = (((1,), (1,)), ((), ()))

    def block(j, carry, masked):
        m1, l1, a1, m2, l2, a2 = carry
        rows = pl.ds(pl.multiple_of(j * tq, tq), tq)
        kb = k_ref[rows, :].astype(BF16)
        vb = v_ref[rows, :].astype(BF16)
        s1 = lax.dot_general(q1, kb[:, :DK], nt, preferred_element_type=F32)
        s2 = lax.dot_general(q2, kb[:, DK:], nt, preferred_element_type=F32)
        if masked:
            r = lax.broadcasted_iota(jnp.int32, (tq, tq), 0)
            c = lax.broadcasted_iota(jnp.int32, (tq, tq), 1)
            keep = c <= r
            s1 = jnp.where(keep, s1, -jnp.inf)
            s2 = jnp.where(keep, s2, -jnp.inf)

        def upd(s, m, l, a):
            mn = jnp.maximum(m, jnp.max(s, axis=1, keepdims=True))
            al = jnp.exp(m - mn)
            p = jnp.exp(s - mn)
            l = al * l + jnp.sum(p, axis=1, keepdims=True)
            a = al * a + jnp.dot(p.astype(BF16), vb, preferred_element_type=F32)
            return mn, l, a

        m1, l1, a1 = upd(s1, m1, l1, a1)
        m2, l2, a2 = upd(s2, m2, l2, a2)
        return m1, l1, a1, m2, l2, a2

    neg = jnp.full((tq, 1), -jnp.inf, F32)
    zl = jnp.zeros((tq, 1), F32)
    za = jnp.zeros((tq, DV), F32)
    carry = lax.fori_loop(0, qi, lambda j, c: block(j, c, False), (neg, zl, za, neg, zl, za))
    m1, l1, a1, m2, l2, a2 = block(qi, carry, True)
    o = a1 / l1 - lam * (a2 / l2)
    o_ref[...] = _sub_rmsnorm(o, g_ref[...], 1.0 - lam_init).astype(o_ref.dtype)


def prompt_attention(h3, lam4, subln_g, *, tq, lam_init):
    B, S, _ = h3.shape
    return pl.pallas_call(
        functools.partial(_pattn_kernel, tq=tq, lam_init=lam_init),
        grid=(B, HA, S // tq),
        in_specs=[pl.BlockSpec((4, DK), lambda b, h, i: (0, 0)),
                  pl.BlockSpec((None, tq, DV), lambda b, h, i: (b, i, h)),
                  pl.BlockSpec((None, S, DV), lambda b, h, i: (b, 0, HA + h)),
                  pl.BlockSpec((None, S, DV), lambda b, h, i: (b, 0, 2 * HA + h)),
                  pl.BlockSpec((1, DV), lambda b, h, i: (0, 0))],
        out_specs=pl.BlockSpec((None, tq, DV), lambda b, h, i: (b, i, h)),
        out_shape=jax.ShapeDtypeStruct((B, S, HA * DV), BF16),
        compiler_params=_cparams(("arbitrary", "arbitrary", "arbitrary")),
        name="prompt_attention",
    )(lam4, h3, h3, h3, subln_g)


def _sattn_kernel(pt_ref, lam_ref, q_ref, kn_ref, vn_ref, ones_ref, g_ref, *rest, lam_init, pp):
    ck_refs, cv_refs = rest[:pp], rest[pp:2 * pp]
    o_ref, m_ref, l_ref, a1_ref, a2_ref = rest[2 * pp:]
    p = pl.program_id(1)
    qs = q_ref[...] * (DK ** -0.5)
    ones = ones_ref[...]

    def twice(x):
        return jnp.concatenate([x, x], axis=-1)

    @pl.when(p == 0)
    def _():
        s_self = jnp.dot((kn_ref[...] * qs).astype(BF16), ones, preferred_element_type=F32)
        m_ref[...] = s_self
        l_ref[...] = jnp.ones_like(s_self)
        a1_ref[...] = vn_ref[...]
        a2_ref[...] = vn_ref[...]

    scores = []
    for ck_ref in ck_refs:
        prod = (ck_ref[...] * qs[None]).astype(BF16).reshape(PAGE_SIZE * HA, 2 * DK)
        s = jnp.dot(prod, ones, preferred_element_type=F32)
        scores.append(s.reshape(PAGE_SIZE, HA, 2 * DK))
    m_old = m_ref[...]
    m_new = m_old
    for s in scores:
        m_new = jnp.maximum(m_new, jnp.max(s, axis=0))
    alpha = jnp.exp(m_old - m_new)
    lsum = jnp.zeros_like(m_old)
    c = [jnp.zeros((HA, DK), F32) for _ in range(4)]
    for s, cv_ref in zip(scores, cv_refs):
        pr = jnp.exp(s - m_new[None])
        lsum = lsum + jnp.sum(pr, axis=0)
        v = cv_ref[...]
        p1, p2 = pr[:, :, :DK], pr[:, :, DK:]
        vlo, vhi = v[:, :, :DK], v[:, :, DK:]
        c[0] = c[0] + jnp.sum(p1 * vlo, axis=0)
        c[1] = c[1] + jnp.sum(p1 * vhi, axis=0)
        c[2] = c[2] + jnp.sum(p2 * vlo, axis=0)
        c[3] = c[3] + jnp.sum(p2 * vhi, axis=0)
    l_ref[...] = alpha * l_ref[...] + lsum
    m_ref[...] = m_new
    al1, al2 = alpha[:, :DK], alpha[:, DK:]
    a1_ref[:, :DK] = al1 * a1_ref[:, :DK] + c[0]
    a1_ref[:, DK:] = al1 * a1_ref[:, DK:] + c[1]
    a2_ref[:, :DK] = al2 * a2_ref[:, :DK] + c[2]
    a2_ref[:, DK:] = al2 * a2_ref[:, DK:] + c[3]

    @pl.when(p == pl.num_programs(1) - 1)
    def _():
        lam = _lambda_value(lam_ref, lam_init)
        l = l_ref[...]
        o = a1_ref[...] / twice(l[:, :DK]) - lam * (a2_ref[...] / twice(l[:, DK:]))
        o_ref[...] = _sub_rmsnorm(o, g_ref[...], 1.0 - lam_init).astype(o_ref.dtype)


def sample_attention(page_table, lam4, hs3, ones_blk, subln_g, cache_k, cache_v, *, lam_init, pp):
    Bd, n_pages = page_table.shape
    assert n_pages % pp == 0
    page = lambda i: (lambda b, p, pt: (pt[b, p * pp + i], 0, 0, 0))
    page_specs = [pl.BlockSpec((None, PAGE_SIZE, HA, DV), page(i)) for i in range(pp)]
    grid_spec = pltpu.PrefetchScalarGridSpec(
        num_scalar_prefetch=1,
        grid=(Bd, n_pages // pp),
        in_specs=[pl.BlockSpec((4, DK), lambda b, p, pt: (0, 0)),
                  pl.BlockSpec((None, HA, DV), lambda b, p, pt: (b, 0, 0)),
                  pl.BlockSpec((None, HA, DV), lambda b, p, pt: (b, 1, 0)),
                  pl.BlockSpec((None, HA, DV), lambda b, p, pt: (b, 2, 0)),
                  pl.BlockSpec((DV, DV), lambda b, p, pt: (0, 0)),
                  pl.BlockSpec((1, DV), lambda b, p, pt: (0, 0))] + page_specs + page_specs,
        out_specs=pl.BlockSpec((None, HA, DV), lambda b, p, pt: (b, 0, 0)),
        scratch_shapes=[pltpu.VMEM((HA, DV), F32)] * 4,
    )
    return pl.pallas_call(
        functools.partial(_sattn_kernel, lam_init=lam_init, pp=pp),
        grid_spec=grid_spec,
        out_shape=jax.ShapeDtypeStruct((Bd, HA, DV), BF16),
        compiler_params=_cparams(("arbitrary", "arbitrary")),
        name="sample_attention",
    )(page_table, lam4, hs3, hs3, hs3, ones_blk, subln_g, *([cache_k] * pp), *([cache_v] * pp))


def _lower_bound(lb_ref, layer):
    z = lb_ref[...]
    e = jnp.exp(z - jnp.max(z, axis=0, keepdims=True))
    return jnp.sum(e[: layer + 1], axis=0, keepdims=True) / jnp.sum(e, axis=0, keepdims=True)


def _split3(x):
    hi = x.astype(BF16)
    r1 = x - hi.astype(F32)
    mid = r1.astype(BF16)
    lo = (r1 - mid.astype(F32)).astype(BF16)
    return hi, mid, lo


def _hgrn_prompt_kernel(q_ref, f_ref, i_ref, g_ref, lb_ref, gn_ref, o_ref, s_ref, st_ref,
                        *, hb, layer, seq):
    C = HG_CHUNK
    mid = C // 2
    lb = _lower_bound(lb_ref, layer)
    r = lax.broadcasted_iota(jnp.int32, (C, C), 0)
    c = lax.broadcasted_iota(jnp.int32, (C, C), 1)
    causal = c <= r
    tri = jnp.where(causal, 1.0, 0.0).astype(BF16)
    st_ref[...] = jnp.zeros_like(st_ref)
    nt = (((1,), (1,)), ((), ()))

    def chunk(ci, _):
        rows = pl.ds(pl.multiple_of(ci * C, C), C)
        for hh in range(hb):
            cols = slice(hh * DKH, (hh + 1) * DKH)
            qc = q_ref[rows, cols]
            f = lb[:, cols] + (1.0 - lb[:, cols]) * jax.nn.sigmoid(f_ref[rows, cols])
            gl = jnp.log(f)
            kc = 1.0 - f
            vc = i_ref[rows, cols]
            G = sum(jnp.dot(tri, t, preferred_element_type=F32) for t in _split3(gl))
            g_mid = G[mid:mid + 1, :]
            g_last = G[C - 1:C, :]
            st = st_ref[hh]
            o_inter = lax.dot_general((qc * jnp.exp(G)).astype(BF16), st.astype(BF16), nt,
                                      preferred_element_type=F32)
            A = lax.dot_general((qc * jnp.exp(G - g_mid)).astype(BF16),
                                (kc * jnp.exp(g_mid - G)).astype(BF16), nt,
                                preferred_element_type=F32)
            A = jnp.where(causal, A, 0.0)
            o = o_inter + jnp.dot(A.astype(BF16), vc.astype(BF16), preferred_element_type=F32)
            kd = (kc * jnp.exp(g_last - G)).astype(BF16)
            st_ref[hh] = jnp.exp(g_last) * st + jnp.dot(vc.T.astype(BF16), kd,
                                                        preferred_element_type=F32)
            og = o * jax.nn.sigmoid(g_ref[rows, cols])
            ms = jnp.mean(og * og, axis=-1, keepdims=True)
            o_ref[rows, cols] = (og * lax.rsqrt(ms + RMS_EPS) * gn_ref[...]).astype(o_ref.dtype)
        return 0

    lax.fori_loop(0, seq // C, chunk, 0)
    for hh in range(hb):
        s_ref[hh] = st_ref[hh].T


def hgrn_prompt(h3, lb_logits, hnorm_g, *, hb, layer):
    B, S, _ = h3.shape
    W = hb * DKH
    nb = HG // hb
    seg = lambda k: (lambda b, j: (b, 0, k * nb + j))
    return pl.pallas_call(
        functools.partial(_hgrn_prompt_kernel, hb=hb, layer=layer, seq=S),
        grid=(B, nb),
        in_specs=[pl.BlockSpec((None, S, W), seg(3)),
                  pl.BlockSpec((None, S, W), seg(4)),
                  pl.BlockSpec((None, S, W), seg(5)),
                  pl.BlockSpec((None, S, W), seg(6)),
                  pl.BlockSpec((lb_logits.shape[0], W), lambda b, j: (0, j)),
                  pl.BlockSpec((1, DVH), lambda b, j: (0, 0))],
        out_specs=[pl.BlockSpec((None, S, W), lambda b, j: (b, 0, j)),
                   pl.BlockSpec((None, hb, DKH, DVH), lambda b, j: (b, j, 0, 0))],
        out_shape=[jax.ShapeDtypeStruct((B, S, HG * DVH), BF16),
                   jax.ShapeDtypeStruct((B, HG, DKH, DVH), F32)],
        scratch_shapes=[pltpu.VMEM((hb, DVH, DKH), F32)],
        compiler_params=_cparams(("arbitrary", "arbitrary")),
        name="hgrn_prompt",
    )(h3, h3, h3, h3, lb_logits, hnorm_g)


def _hgrn_step_kernel(q_ref, f_ref, i_ref, g_ref, lb_ref, gn_ref, s0_ref, o_ref, s_ref, *, layer):
    z = lb_ref[...]
    e = jnp.exp(z - jnp.max(z, axis=0, keepdims=True))
    lb = jnp.sum(e[: layer + 1], axis=0) / jnp.sum(e, axis=0)
    q = q_ref[...]
    f = lb + (1.0 - lb) * jax.nn.sigmoid(f_ref[...])
    k = 1.0 - f
    v = i_ref[...]
    qk = jnp.sum(q * k, axis=1, keepdims=True)
    fT = f.T
    kT = k.T
    qfT = (q * f).T
    for h in range(HG):
        S = s0_ref[h]
        vh = v[h:h + 1, :]
        s_ref[h] = fT[:, h:h + 1] * S + kT[:, h:h + 1] * vh
        o = jnp.sum(qfT[:, h:h + 1] * S, axis=0, keepdims=True) + qk[h:h + 1, :] * vh
        og = o * jax.nn.sigmoid(g_ref[h:h + 1, :])
        ms = jnp.mean(og * og, axis=-1, keepdims=True)
        o_ref[h:h + 1, :] = (og * lax.rsqrt(ms + RMS_EPS) * gn_ref[...]).astype(o_ref.dtype)


def hgrn_step(hs4, lb3, hnorm_g, s0, *, layer):
    Bd = hs4.shape[0]
    seg = lambda k: (lambda b: (b, k, 0, 0))
    return pl.pallas_call(
        functools.partial(_hgrn_step_kernel, layer=layer),
        grid=(Bd,),
        in_specs=[pl.BlockSpec((None, None, HG, DKH), seg(3)),
                  pl.BlockSpec((None, None, HG, DKH), seg(4)),
                  pl.BlockSpec((None, None, HG, DVH), seg(5)),
                  pl.BlockSpec((None, None, HG, DVH), seg(6)),
                  pl.BlockSpec(lb3.shape, lambda b: (0, 0, 0)),
                  pl.BlockSpec((1, DVH), lambda b: (0, 0)),
                  pl.BlockSpec((None, HG, DKH, DVH), lambda b: (b, 0, 0, 0))],
        out_specs=[pl.BlockSpec((None, HG, DVH), lambda b: (b, 0, 0)),
                   pl.BlockSpec((None, HG, DKH, DVH), lambda b: (b, 0, 0, 0))],
        out_shape=[jax.ShapeDtypeStruct((Bd, HG, DVH), BF16),
                   jax.ShapeDtypeStruct((Bd, HG, DKH, DVH), F32)],
        compiler_params=_cparams(("arbitrary",)),
        name="hgrn_step",
    )(hs4, hs4, hs4, hs4, lb3, hnorm_g, s0)


def _layer_norm(z, g, b):
    mu = jnp.mean(z, axis=-1, keepdims=True)
    zc = z - mu
    var = jnp.mean(zc * zc, axis=-1, keepdims=True)
    return zc * lax.rsqrt(var + LN_EPS) * g + b


def _ln_router_kernel(z_ref, g_ref, b_ref, wr_ref, br_ref, c0_ref, xbuf_hbm, x_ref, ti_ref, tg_ref,
                      rk_ref, cnt_ref, base_ref):
    del xbuf_hbm

    @pl.when(pl.program_id(0) == 0)
    def _():
        base_ref[...] = c0_ref[...]

    x = _layer_norm(z_ref[...], g_ref[...], b_ref[...])
    x_ref[...] = x
    w_hi, w_mid, w_lo = _split3(wr_ref[...])
    x_hi, x_mid, x_lo = _split3(x)
    d = lambda a, w: jnp.dot(a, w, preferred_element_type=F32)
    logits = (d(x_hi, w_hi) + (d(x_hi, w_mid) + d(x_mid, w_hi))
              + (d(x_hi, w_lo) + d(x_mid, w_mid) + d(x_lo, w_hi))) + br_ref[...]
    E = logits.shape[1]
    lane = lax.broadcasted_iota(jnp.int32, logits.shape, 1)
    col = lax.broadcasted_iota(jnp.int32, (logits.shape[0], TOP_K), 1)
    tm = logits.shape[0]
    work = logits
    top_i = jnp.zeros((tm, TOP_K), jnp.int32)
    top_v = jnp.zeros((tm, TOP_K), F32)
    picks = []
    for r in range(TOP_K):
        mx = jnp.max(work, axis=1, keepdims=True)
        idx = jnp.min(jnp.where(work == mx, lane, E), axis=1, keepdims=True)
        top_i = jnp.where(col == r, idx, top_i)
        top_v = jnp.where(col == r, mx, top_v)
        picks.append(lane == idx)
        work = jnp.where(picks[-1], -jnp.inf, work)
    ex = jnp.exp(top_v - top_v[:, 0:1])
    ti_ref[...] = top_i
    tg_ref[...] = ex / jnp.sum(ex, axis=1, keepdims=True)
    onehot = sum(jnp.where(pk, 1.0, 0.0) for pk in picks)
    rr = lax.broadcasted_iota(jnp.int32, (tm, tm), 0)
    cc = lax.broadcasted_iota(jnp.int32, (tm, tm), 1)
    before = jnp.where(cc < rr, 1.0, 0.0).astype(BF16)
    rank_all = base_ref[...] + jnp.dot(before, onehot.astype(BF16), preferred_element_type=F32)
    rank = jnp.zeros((tm, TOP_K), F32)
    for r in range(TOP_K):
        rk = jnp.sum(jnp.where(picks[r], rank_all, 0.0), axis=1, keepdims=True)
        rank = jnp.where(col == r, rk, rank)
    rk_ref[...] = rank.astype(jnp.int32)
    base_ref[...] = base_ref[...] + jnp.sum(onehot, axis=0, keepdims=True)
    cnt_ref[...] = base_ref[...]


def ln_router(z, g, b, w_router, b_router, counts0, *, tm, x_buf, row_off=0):
    M, D = z.shape
    E = w_router.shape[1]
    assert M % tm == 0 and row_off % tm == 0
    ro = row_off // tm
    row = lambda i: (i, 0)
    fix = lambda i: (0, 0)
    return pl.pallas_call(
        _ln_router_kernel,
        grid=(M // tm,),
        in_specs=[pl.BlockSpec((tm, D), row), pl.BlockSpec((1, D), fix), pl.BlockSpec((1, D), fix),
                  pl.BlockSpec((D, E), fix), pl.BlockSpec((1, E), fix), pl.BlockSpec((1, E), fix),
                  pl.BlockSpec(memory_space=pl.ANY)],
        out_specs=[pl.BlockSpec((tm, D), lambda i: (ro + i, 0)),
                   pl.BlockSpec((tm, TOP_K), row), pl.BlockSpec((tm, TOP_K), row),
                   pl.BlockSpec((tm, TOP_K), row), pl.BlockSpec((1, E), fix)],
        out_shape=[jax.ShapeDtypeStruct(x_buf.shape, F32),
                   jax.ShapeDtypeStruct((M, TOP_K), jnp.int32),
                   jax.ShapeDtypeStruct((M, TOP_K), F32),
                   jax.ShapeDtypeStruct((M, TOP_K), jnp.int32),
                   jax.ShapeDtypeStruct((1, E), F32)],
        scratch_shapes=[pltpu.VMEM((1, E), F32)],
        input_output_aliases={6: 0},
        compiler_params=_cparams(("arbitrary",)),
        name="ln_router",
    )(z, g, b, w_router, b_router, counts0, x_buf)


def _combine_ln_kernel(sp_ref, x_ref, tg_ref, g_ref, b_ref, ys_hbm, o_ref, buf_ref, sem,
                       *, tm, tok_off):
    base = (tok_off + pl.program_id(0) * tm) * TOP_K

    def row_copy(r, k):
        slot = sp_ref[base + r * TOP_K + k]
        return pltpu.make_async_copy(ys_hbm.at[pl.ds(slot, 1), :],
                                     buf_ref.at[k, pl.ds(r, 1), :], sem)

    def start_row(r, _):
        for k in range(TOP_K):
            row_copy(r, k).start()
        return 0

    def wait_row(r, _):
        for k in range(TOP_K):
            row_copy(r, k).wait()
        return 0

    lax.fori_loop(0, tm, start_row, 0)
    lax.fori_loop(0, tm, wait_row, 0)
    tg = tg_ref[...]
    y = sum(tg[:, k:k + 1] * buf_ref[k] for k in range(TOP_K))
    o_ref[...] = _layer_norm(DN_ALPHA * x_ref[...] + y, g_ref[...], b_ref[...])


def combine_ln(slot_pair, x_all, top_g, g, b, ys, *, tm, n_rows, row_off):
    D = x_all.shape[1]
    assert n_rows % tm == 0 and row_off % tm == 0
    ro = row_off // tm
    fix = lambda i, sp: (0, 0)
    grid_spec = pltpu.PrefetchScalarGridSpec(
        num_scalar_prefetch=1,
        grid=(n_rows // tm,),
        in_specs=[pl.BlockSpec((tm, D), lambda i, sp: (ro + i, 0)),
                  pl.BlockSpec((tm, TOP_K), lambda i, sp: (ro + i, 0)),
                  pl.BlockSpec((1, D), fix), pl.BlockSpec((1, D), fix),
                  pl.BlockSpec(memory_space=pl.ANY)],
        out_specs=pl.BlockSpec((tm, D), lambda i, sp: (i, 0)),
        scratch_shapes=[pltpu.VMEM((TOP_K, tm, D), F32), pltpu.SemaphoreType.DMA(())],
    )
    return pl.pallas_call(
        functools.partial(_combine_ln_kernel, tm=tm, tok_off=row_off),
        grid_spec=grid_spec,
        out_shape=jax.ShapeDtypeStruct((n_rows, D), F32),
        compiler_params=_cparams(("arbitrary",)),
        name="combine_ln",
    )(slot_pair, x_all, top_g, g, b, ys)


def _row_chunks(cnt):
    return (cnt + (MOE_CHUNK - 1)) // MOE_CHUNK


def _for_row_blocks(n_chunks, fn):
    pairs = n_chunks // 2

    def body(c, _):
        fn(pl.multiple_of(c * (2 * MOE_CHUNK), 2 * MOE_CHUNK), 2 * MOE_CHUNK)
        return 0

    lax.fori_loop(0, pairs, body, 0)

    @pl.when(n_chunks % 2 == 1)
    def _():
        fn(pl.multiple_of(pairs * (2 * MOE_CHUNK), MOE_CHUNK), MOE_CHUNK)


def _zero_tail(o_ref, n_chunks):
    def body(c, _):
        rows = pl.ds(pl.multiple_of(c * MOE_CHUNK, MOE_CHUNK), MOE_CHUNK)
        o_ref[rows, :] = jnp.zeros((MOE_CHUNK, o_ref.shape[1]), o_ref.dtype)
        return 0

    lax.fori_loop(n_chunks, MOE_GROUP_ROWS // MOE_CHUNK, body, 0)


def _moe_up_kernel(ge_ref, gp_ref, gc_ref, tok_ref, x_hbm, wg_ref, wu_ref, bg_ref, bu_ref,
                   o_ref, xbuf_ref, stage_ref, wgb_ref, wub_ref, sem, *, n_pairs):
    g = pl.program_id(0)
    cnt = gc_ref[g]
    pos = gp_ref[g]
    n_chunks = _row_chunks(cnt)

    def row_copy(c, r, slot):
        q = c * MOE_CHUNK + r
        tok = jnp.where(q < cnt, tok_ref[jnp.minimum(pos + q, n_pairs - 1)], 0)
        return pltpu.make_async_copy(x_hbm.at[pl.ds(tok, 1), :],
                                     stage_ref.at[slot, pl.ds(r, 1), :], sem.at[slot])

    @pl.when((pl.program_id(1) == 0) & (cnt > 0))
    def _():
        def start_chunk(c, slot):
            lax.fori_loop(0, MOE_CHUNK, lambda r, _: (row_copy(c, r, slot).start(), 0)[1], 0)

        def wait_chunk(c, slot):
            lax.fori_loop(0, MOE_CHUNK, lambda r, _: (row_copy(c, r, slot).wait(), 0)[1], 0)

        start_chunk(0, 0)

        def chunk(c, _):
            slot = lax.rem(c, 2)

            @pl.when(c + 1 < n_chunks)
            def _():
                start_chunk(c + 1, 1 - slot)

            wait_chunk(c, slot)
            rows = pl.ds(pl.multiple_of(c * MOE_CHUNK, MOE_CHUNK), MOE_CHUNK)
            xbuf_ref[rows, :] = stage_ref[slot].astype(BF16)
            return 0

        lax.fori_loop(0, n_chunks, chunk, 0)

    @pl.when(cnt > 0)
    def _():
        wgb_ref[...] = wg_ref[...].astype(BF16)
        wub_ref[...] = wu_ref[...].astype(BF16)

        def block(r0, n):
            x = xbuf_ref[pl.ds(r0, n), :]
            a = jnp.dot(x, wgb_ref[...], preferred_element_type=F32) + bg_ref[...]
            u = jnp.dot(x, wub_ref[...], preferred_element_type=F32) + bu_ref[...]
            a = jnp.minimum(a, SWIGLU_LIMIT)
            u = jnp.clip(u, -SWIGLU_LIMIT, SWIGLU_LIMIT)
            h = (u + 1.0) * a * jax.nn.sigmoid(SWIGLU_ALPHA * a)
            o_ref[pl.ds(r0, n), :] = h.astype(o_ref.dtype)

        _for_row_blocks(n_chunks, block)

    _zero_tail(o_ref, n_chunks)


def moe_up(grp_e, grp_pos, grp_cnt, tok_sorted, x_all, w_gate_up, b_gate_up3, *, tn):
    D = x_all.shape[1]
    E, _, F2 = w_gate_up.shape
    F = F2 // 2
    nj = F // tn
    G = grp_e.shape[0]
    R = MOE_GROUP_ROWS
    jj = lambda j, gc, g: jnp.where(gc[g] > 0, j, nj - 1)
    grid_spec = pltpu.PrefetchScalarGridSpec(
        num_scalar_prefetch=4,
        grid=(G, nj),
        in_specs=[pl.BlockSpec(memory_space=pl.ANY),
                  pl.BlockSpec((None, D, tn), lambda g, j, ge, gp, gc, tk: (ge[g], 0, jj(j, gc, g))),
                  pl.BlockSpec((None, D, tn),
                               lambda g, j, ge, gp, gc, tk: (ge[g], 0, nj + jj(j, gc, g))),
                  pl.BlockSpec((None, 1, tn), lambda g, j, ge, gp, gc, tk: (ge[g], 0, jj(j, gc, g))),
                  pl.BlockSpec((None, 1, tn),
                               lambda g, j, ge, gp, gc, tk: (ge[g], 0, nj + jj(j, gc, g)))],
        out_specs=pl.BlockSpec((R, tn), lambda g, j, ge, gp, gc, tk: (g, j)),
        scratch_shapes=[pltpu.VMEM((R, D), BF16), pltpu.VMEM((2, MOE_CHUNK, D), F32),
                        pltpu.VMEM((D, tn), BF16), pltpu.VMEM((D, tn), BF16),
                        pltpu.SemaphoreType.DMA((2,))],
    )
    return pl.pallas_call(
        functools.partial(_moe_up_kernel, n_pairs=tok_sorted.shape[0]),
        grid_spec=grid_spec,
        out_shape=jax.ShapeDtypeStruct((G * R, F), BF16),
        compiler_params=_cparams(("arbitrary", "arbitrary")),
        name="moe_up",
    )(grp_e, grp_pos, grp_cnt, tok_sorted, x_all, w_gate_up, w_gate_up,
      b_gate_up3, b_gate_up3)


def _moe_down_kernel(ge_ref, gc_ref, gb_ref, h_ref, wd_ref, bd_ref, o_ref, wdb_ref):
    cnt = gc_ref[pl.program_id(0)]
    n_chunks = _row_chunks(cnt)

    @pl.when(cnt > 0)
    def _():
        wdb_ref[...] = wd_ref[...].astype(BF16)

        def block(r0, n):
            y = jnp.dot(h_ref[pl.ds(r0, n), :], wdb_ref[...], preferred_element_type=F32)
            o_ref[pl.ds(r0, n), :] = y + bd_ref[...]

        _for_row_blocks(n_chunks, block)

    _zero_tail(o_ref, n_chunks)


def moe_down(grp_e, grp_cnt, grp_in, hdn, w_down, b_down3, *, tn):
    F = hdn.shape[1]
    D = w_down.shape[2]
    nj = D // tn
    G = grp_e.shape[0]
    R = MOE_GROUP_ROWS
    jj = lambda j, gc, g: jnp.where(gc[g] > 0, j, nj - 1)
    grid_spec = pltpu.PrefetchScalarGridSpec(
        num_scalar_prefetch=3,
        grid=(G, nj),
        in_specs=[pl.BlockSpec((R, F), lambda g, j, ge, gc, gb: (gb[g], 0)),
                  pl.BlockSpec((None, F, tn), lambda g, j, ge, gc, gb: (ge[g], 0, jj(j, gc, g))),
                  pl.BlockSpec((None, 1, tn), lambda g, j, ge, gc, gb: (ge[g], 0, jj(j, gc, g)))],
        out_specs=pl.BlockSpec((R, tn), lambda g, j, ge, gc, gb: (g, j)),
        scratch_shapes=[pltpu.VMEM((F, tn), BF16)],
    )
    return pl.pallas_call(
        _moe_down_kernel,
        grid_spec=grid_spec,
        out_shape=jax.ShapeDtypeStruct((G * R, D), F32),
        compiler_params=_cparams(("arbitrary", "arbitrary")),
        name="moe_down",
    )(grp_e, grp_cnt, grp_in, hdn, w_down, b_down3)


def _route(top_i, rank, counts):
    n_tok = top_i.shape[0]
    E, R = N_EXPERTS, MOE_GROUP_ROWS
    NK = n_tok * TOP_K
    G = E + NK // R
    i32 = jnp.int32
    ar_e = jnp.arange(E, dtype=i32)
    cnt = counts.reshape(E).astype(i32)
    ng = (cnt + R - 1) // R
    gend = jnp.cumsum(ng).astype(i32)
    gfirst = gend - ng
    n_groups = gend[-1]
    start = (jnp.cumsum(cnt) - cnt).astype(i32)
    g = jnp.arange(G, dtype=i32)
    valid = g < n_groups
    ge_raw = jnp.sum(g[:, None] >= gend[None, :], axis=1).astype(i32)
    e_last = jnp.sum((n_groups - 1) >= gend).astype(i32)
    ge = jnp.where(valid, jnp.minimum(ge_raw, E - 1), e_last)
    pick = lambda oh, tbl: jnp.sum(jnp.where(oh, tbl, 0), axis=-1).astype(i32)
    oh_g = ge[:, None] == ar_e[None, :]
    row0 = (g - pick(oh_g, gfirst)) * R
    gcnt = jnp.where(valid, jnp.clip(pick(oh_g, cnt) - row0, 0, R), 0).astype(i32)
    gpos = jnp.where(valid, pick(oh_g, start) + row0, 0).astype(i32)
    gin = jnp.where(valid, g, n_groups - 1).astype(i32)
    oh_p = top_i[:, :, None] == ar_e[None, None, :]
    slot_pair = (pick(oh_p, gfirst) + rank // R) * R + rank % R
    pos_pair = pick(oh_p, start) + rank
    tok = jnp.broadcast_to(jnp.arange(n_tok, dtype=i32)[:, None], (n_tok, TOP_K))
    tok_sorted = jnp.zeros((NK,), i32).at[pos_pair.reshape(-1)].set(tok.reshape(-1))
    return ge, gpos, gcnt, gin, tok_sorted, slot_pair.reshape(-1).astype(i32)


def kernel(x_prompt, x_sample, cache_k, cache_v, state_hgrn, page_table, w_in, w_out, lambda_q1, lambda_k1, lambda_q2, lambda_k2, subln_g, hgrn_lb_logits, hgrn_norm_g, ln1_g, ln1_b, w_router, b_router, w_gate_up, b_gate_up, w_down, b_down, ln2_g, ln2_b):
    B, S, D = x_prompt.shape
    Bd = x_sample.shape[0]
    l = 0
    lam_init = 0.8 - 0.6 * math.exp(-0.3 * l)
    d_in = w_in.shape[2]
    wa = HA * DV

    xp = x_prompt.reshape(B * S, D)
    xd = x_sample.reshape(Bd, D)
    lam4 = jnp.stack([lambda_q1[l], lambda_k1[l], lambda_q2[l], lambda_k2[l]])
    g_sub = subln_g[l].reshape(1, DV)
    g_hn = hgrn_norm_g[l].reshape(1, DVH)

    hp = matmul(xp.astype(BF16), w_in[l], tm=1024, tn=512)
    hd = matmul(xd.astype(BF16), w_in[l], tm=Bd, tn=512)
    hp3 = hp.reshape(B, S, d_in)

    oa_p = prompt_attention(hp3, lam4, g_sub, tq=256, lam_init=lam_init)
    oh_p, s_p = hgrn_prompt(hp3, hgrn_lb_logits, g_hn, hb=2, layer=l)

    r = lax.broadcasted_iota(jnp.int32, (DV, DV), 0) // DK
    c = lax.broadcasted_iota(jnp.int32, (DV, DV), 1) // DK
    ones_blk = (r == c).astype(BF16)
    oa_d = sample_attention(page_table, lam4, hd.reshape(Bd, d_in // DV, DV), ones_blk, g_sub,
                            cache_k[l], cache_v[l], lam_init=lam_init, pp=4)
    lb3 = hgrn_lb_logits.reshape(hgrn_lb_logits.shape[0], HG, DKH)
    oh_d, s_d = hgrn_step(hd.reshape(Bd, d_in // (HG * DKH), HG, DKH), lb3, g_hn,
                          state_hgrn[l], layer=l)

    N = B * S + Bd
    mix_p = jnp.concatenate([oa_p.reshape(B * S, wa), oh_p.reshape(B * S, HG * DVH)], axis=1)
    mix_d = jnp.concatenate([oa_d.reshape(Bd, wa), oh_d.reshape(Bd, HG * DVH)], axis=1)
    zp = matmul(mix_p, w_out[l], tm=1024, tn=512, res=xp, alpha=DN_ALPHA)
    zd = matmul(mix_d, w_out[l], tm=Bd, tn=512, res=xd, alpha=DN_ALPHA)
    ln1 = (ln1_g[l].reshape(1, D), ln1_b[l].reshape(1, D), w_router[l], b_router[l].reshape(1, -1))
    zero_counts = jnp.zeros((1, N_EXPERTS), F32)
    x1 = jnp.zeros((N, D), F32)
    x1, ti_p, tg_p, rk_p, cnt_p = ln_router(zp, *ln1, zero_counts, tm=256, x_buf=x1)
    x1, ti_d, tg_d, rk_d, counts = ln_router(zd, *ln1, cnt_p, tm=Bd, row_off=B * S, x_buf=x1)
    top_i = jnp.concatenate([ti_p, ti_d], axis=0)
    top_g = jnp.concatenate([tg_p, tg_d], axis=0)
    rank = jnp.concatenate([rk_p, rk_d], axis=0)

    grp_e, grp_pos, grp_cnt, grp_in, tok_sorted, slot_pair = _route(top_i, rank, counts)
    hdn = moe_up(grp_e, grp_pos, grp_cnt, tok_sorted, x1, w_gate_up[l],
                 b_gate_up[l].reshape(N_EXPERTS, 1, -1), tn=min(256, w_gate_up.shape[3] // 2))
    ys = moe_down(grp_e, grp_cnt, grp_in, hdn, w_down[l],
                  b_down[l].reshape(N_EXPERTS, 1, -1), tn=min(512, D))
    ln2 = (ln2_g[l].reshape(1, D), ln2_b[l].reshape(1, D))
    y_p = combine_ln(slot_pair, x1, top_g, *ln2, ys, tm=128, n_rows=B * S, row_off=0)
    y_d = combine_ln(slot_pair, x1, top_g, *ln2, ys, tm=Bd, n_rows=Bd, row_off=B * S)
    y_p = y_p.reshape(B, S, D)
    y_d = y_d.reshape(Bd, 1, D)

    k_p = hp[:, wa:2 * wa].reshape(1, B, S, HA, DV)
    v_p = hp[:, 2 * wa:3 * wa].reshape(1, B, S, HA, DV)
    k_d = hd[:, wa:2 * wa].reshape(1, Bd, 1, HA, DV)
    v_d = hd[:, 2 * wa:3 * wa].reshape(1, Bd, 1, HA, DV)
    return (y_p, y_d, k_p, v_p, s_p[None], k_d, v_d, s_d[None])
```
